```python
import math
import jax, jax.numpy as jnp
from jax import lax
import numpy as np

D_MODEL = 2048
BATCH = 2
SEQ = 4096
DEPTH = 1

GRID_W = 64
HEAD_DIM = 128
NA_HEADS = 8
NA_WIDTH = NA_HEADS * HEAD_DIM
NA_WIN_ROWS = 8
NA_WIN_COLS = 16
DIFF_HEADS = 4
DIFF_QK_DIM = HEAD_DIM
DIFF_V_DIM = 2 * HEAD_DIM
DIFF_QK_WIDTH = DIFF_HEADS * 2 * DIFF_QK_DIM
DIFF_WIDTH = DIFF_HEADS * DIFF_V_DIM
MIX_WIDTH = NA_WIDTH + DIFF_WIDTH
IN_WIDTH = 3 * NA_WIDTH + 2 * DIFF_QK_WIDTH + DIFF_WIDTH
D_FF = -(-8 * D_MODEL // (3 * 256)) * 256
REL_BUCKETS = 32
REL_MAX_DIST = 128
Q_BLOCK = 128
LN_EPS = 1e-5
DEEPNORM_ALPHA = (2.0 * DEPTH) ** 0.25
DEEPNORM_BETA = (8.0 * DEPTH) ** -0.25

kernel_name = "hybrid_natten_diffattn_deepnorm_encoder"


def layer_norm(x, g, b):
    xf = x.astype(jnp.float32)
    mu = jnp.mean(xf, axis=-1, keepdims=True)
    var = jnp.mean(jnp.square(xf - mu), axis=-1, keepdims=True)
    return ((xf - mu) * lax.rsqrt(var + LN_EPS) * g.astype(jnp.float32) + b.astype(jnp.float32)).astype(x.dtype)


def rms_norm(x, g):
    xf = x.astype(jnp.float32)
    return (xf * lax.rsqrt(jnp.mean(jnp.square(xf), axis=-1, keepdims=True) + LN_EPS) * g.astype(jnp.float32)).astype(x.dtype)


def t5_bucket(rel):
    nb = REL_BUCKETS // 2
    max_exact = nb // 2
    ret = (rel > 0).astype(jnp.int32) * nb
    n = jnp.abs(rel)
    nf = jnp.maximum(n, 1).astype(jnp.float32)
    large = max_exact + (jnp.log(nf / max_exact) / math.log(REL_MAX_DIST / max_exact) * (nb - max_exact)).astype(jnp.int32)
    large = jnp.minimum(large, nb - 1)
    return ret + jnp.where(n < max_exact, n, large)


def neighbourhood_attention(q, k, v, rpb):
    B, S, H, d = q.shape
    rows = S // GRID_W
    kh = min(NA_WIN_ROWS, rows)
    kw = NA_WIN_COLS
    qg = (q * (d ** -0.5)).reshape(B, rows, GRID_W, H, d)
    kg = k.reshape(B, rows, GRID_W, H, d)
    vg = v.reshape(B, rows, GRID_W, H, d)
    cols = jnp.arange(GRID_W)
    c_start = jnp.clip(cols - kw // 2, 0, GRID_W - kw)
    c_idx = c_start[:, None] + jnp.arange(kw)[None, :]
    dc = c_idx - cols[:, None] + (NA_WIN_COLS - 1)
    rpb_c = rpb[:, :, dc]

    def row_block(r):
        r_start = jnp.clip(r - kh // 2, 0, rows - kh)
        q_r = lax.dynamic_index_in_dim(qg, r, axis=1, keepdims=False)
        k_band = lax.dynamic_slice_in_dim(kg, r_start, kh, axis=1)
        v_band = lax.dynamic_slice_in_dim(vg, r_start, kh, axis=1)
        k_n = k_band[:, :, c_idx]
        v_n = v_band[:, :, c_idx]
        logits = jnp.einsum('bchd,brckhd->bhcrk', q_r, k_n).astype(jnp.float32)
        dr = r_start + jnp.arange(kh) - r + (NA_WIN_ROWS - 1)
        bias = jnp.transpose(rpb_c[:, dr], (0, 2, 1, 3)).astype(jnp.float32)
        logits = logits + bias[None]
        p = jax.nn.softmax(logits.reshape(B, H, GRID_W, kh * kw), axis=-1)
        p = p.reshape(B, H, GRID_W, kh, kw).astype(v.dtype)
        return jnp.einsum('bhcrk,brckhd->bchd', p, v_n)

    out = lax.map(row_block, jnp.arange(rows))
    return jnp.transpose(out, (1, 0, 2, 3, 4)).reshape(B, S, H * d)


def differential_attention(q, k, v, lam, rel_table):
    B, S, H, _, dq = q.shape
    dv = v.shape[-1]
    nblk = S // Q_BLOCK
    qb = jnp.transpose((q * (dq ** -0.5)).reshape(B, nblk, Q_BLOCK, H, 2, dq), (1, 0, 2, 3, 4, 5))
    k_pos = jnp.arange(S, dtype=jnp.int32)

    def q_block(args):
        q_i, i = args
        q_pos = i * Q_BLOCK + jnp.arange(Q_BLOCK, dtype=jnp.int32)
        bias = rel_table[t5_bucket(k_pos[None, :] - q_pos[:, None])]
        bias = jnp.transpose(bias, (2, 0, 1)).astype(jnp.float32)
        logits = jnp.einsum('bqhmd,bkhmd->bhmqk', q_i, k).astype(jnp.float32) + bias[None, :, None]
        p = jax.nn.softmax(logits, axis=-1)
        w = (p[:, :, 0] - lam * p[:, :, 1]).astype(v.dtype)
        return jnp.einsum('bhqk,bkhd->bqhd', w, v)

    out = lax.map(q_block, (qb, jnp.arange(nblk, dtype=jnp.int32)))
    return jnp.transpose(out, (1, 0, 2, 3, 4)).reshape(B, S, H, dv)


def setup_inputs(seed: int = 0) -> dict:
    key = jax.random.key(seed)
    ks = jax.random.split(key, 20)
    f32 = jnp.float32
    beta = DEEPNORM_BETA
    col_scale = np.ones((IN_WIDTH,), dtype=np.float32)
    col_scale[2 * NA_WIDTH:3 * NA_WIDTH] = beta
    col_scale[IN_WIDTH - DIFF_WIDTH:] = beta
    w_in = jax.random.normal(ks[1], (DEPTH, D_MODEL, IN_WIDTH), f32) * (D_MODEL ** -0.5) * jnp.asarray(col_scale)
    return {
        'x': jax.random.normal(ks[0], (BATCH, SEQ, D_MODEL), f32),
        'ln_in_g': 1.0 + 0.02 * jax.random.normal(ks[2], (D_MODEL,), f32),
        'ln_in_b': 0.02 * jax.random.normal(ks[3], (D_MODEL,), f32),
        'w_in': w_in,
        'na_rpb': 0.1 * jax.random.normal(ks[4], (DEPTH, NA_HEADS, 2 * NA_WIN_ROWS - 1, 2 * NA_WIN_COLS - 1), f32),
        'lambda_q1': 0.1 * jax.random.normal(ks[5], (DEPTH, DIFF_QK_DIM), f32),
        'lambda_k1': 0.1 * jax.random.normal(ks[6], (DEPTH, DIFF_QK_DIM), f32),
        'lambda_q2': 0.1 * jax.random.normal(ks[7], (DEPTH, DIFF_QK_DIM), f32),
        'lambda_k2': 0.1 * jax.random.normal(ks[8], (DEPTH, DIFF_QK_DIM), f32),
        'diff_subln_g': 1.0 + 0.02 * jax.random.normal(ks[9], (DEPTH, DIFF_V_DIM), f32),
        'rel_bias_table': 0.1 * jax.random.normal(ks[10], (REL_BUCKETS, DIFF_HEADS), f32),
        'w_out': jax.random.normal(ks[11], (DEPTH, MIX_WIDTH, D_MODEL), f32) * (MIX_WIDTH ** -0.5) * beta,
        'ln1_g': 1.0 + 0.02 * jax.random.normal(ks[12], (DEPTH, D_MODEL), f32),
        'ln1_b': 0.02 * jax.random.normal(ks[13], (DEPTH, D_MODEL), f32),
        'w_gate': jax.random.normal(ks[14], (DEPTH, D_MODEL, D_FF), f32) * (D_MODEL ** -0.5) * beta,
        'w_up': jax.random.normal(ks[15], (DEPTH, D_MODEL, D_FF), f32) * (D_MODEL ** -0.5) * beta,
        'w_down': jax.random.normal(ks[16], (DEPTH, D_FF, D_MODEL), f32) * (D_FF ** -0.5) * beta,
        'ln2_g': 1.0 + 0.02 * jax.random.normal(ks[17], (DEPTH, D_MODEL), f32),
        'ln2_b': 0.02 * jax.random.normal(ks[18], (DEPTH, D_MODEL), f32),
    }


def reference(x, ln_in_g, ln_in_b, w_in, na_rpb, lambda_q1, lambda_k1, lambda_q2, lambda_k2,
              diff_subln_g, rel_bias_table, w_out, ln1_g, ln1_b, w_gate, w_up, w_down, ln2_g, ln2_b):
    B, S, _ = x.shape
    splits = [NA_WIDTH, 2 * NA_WIDTH, 3 * NA_WIDTH, 3 * NA_WIDTH + DIFF_QK_WIDTH, 3 * NA_WIDTH + 2 * DIFF_QK_WIDTH]
    h = layer_norm(x, ln_in_g, ln_in_b)
    for l in range(DEPTH):
        proj = jnp.einsum('bsd,de->bse', h, w_in[l])
        na_q, na_k, na_v, d_q, d_k, d_v = jnp.split(proj, splits, axis=-1)
        na_out = neighbourhood_attention(na_q.reshape(B, S, NA_HEADS, HEAD_DIM),
                                         na_k.reshape(B, S, NA_HEADS, HEAD_DIM),
                                         na_v.reshape(B, S, NA_HEADS, HEAD_DIM), na_rpb[l])
        lambda_init = 0.8 - 0.6 * math.exp(-0.3 * l)
        lam = (jnp.exp(jnp.sum(lambda_q1[l].astype(jnp.float32) * lambda_k1[l].astype(jnp.float32)))
               - jnp.exp(jnp.sum(lambda_q2[l].astype(jnp.float32) * lambda_k2[l].astype(jnp.float32)))
               + lambda_init)
        diff_out = differential_attention(d_q.reshape(B, S, DIFF_HEADS, 2, DIFF_QK_DIM),
                                          d_k.reshape(B, S, DIFF_HEADS, 2, DIFF_QK_DIM),
                                          d_v.reshape(B, S, DIFF_HEADS, DIFF_V_DIM), lam, rel_bias_table)
        diff_out = (rms_norm(diff_out, diff_subln_g[l]) * (1.0 - lambda_init)).reshape(B, S, DIFF_WIDTH)
        mixed = jnp.einsum('bse,ed->bsd', jnp.concatenate([na_out, diff_out], axis=-1), w_out[l])
        h = layer_norm(DEEPNORM_ALPHA * h + mixed, ln1_g[l], ln1_b[l])
        ffn = jnp.einsum('bsf,fd->bsd', jax.nn.silu(jnp.einsum('bsd,df->bsf', h, w_gate[l]))
                         * jnp.einsum('bsd,df->bsf', h, w_up[l]), w_down[l])
        h = layer_norm(DEEPNORM_ALPHA * h + ffn, ln2_g[l], ln2_b[l])
    return h
```

```python
import functools
import math

import jax
import jax.numpy as jnp
from jax import lax
from jax.experimental import pallas as pl
from jax.experimental.pallas import tpu as pltpu

F32 = jnp.float32
BF16 = jnp.bfloat16

GRID_W = 64
HEAD_DIM = 128
NA_HEADS = 8
NA_WIN_ROWS = 8
NA_WIN_COLS = 16
DIFF_HEADS = 4
DIFF_QK_DIM = HEAD_DIM
DIFF_V_DIM = 2 * HEAD_DIM
REL_BUCKETS = 32
REL_MAX_DIST = 128
LN_EPS = 1e-5
DEPTH = 1
DEEPNORM_ALPHA = (2.0 * DEPTH) ** 0.25
MASK_VALUE = -1e30

VMEM_LIMIT_BYTES = 56 * 1024 * 1024


def _layer_norm_rows(x, g, b):
    mu = jnp.mean(x, axis=-1, keepdims=True)
    xc = x - mu
    var = jnp.mean(xc * xc, axis=-1, keepdims=True)
    return xc * lax.rsqrt(var + LN_EPS) * g + b


def _ln_inproj_kernel(x_ref, g_ref, b_ref, w_ref, o_ref, h_ref, *, q_scale, q_blocks):
    j = pl.program_id(1)

    @pl.when(j == 0)
    def _():
        h_ref[...] = _layer_norm_rows(x_ref[...], g_ref[...], b_ref[...]).astype(BF16)

    acc = jnp.dot(h_ref[...], w_ref[...], preferred_element_type=F32)
    is_q = functools.reduce(jnp.logical_or, [jnp.logical_and(j >= lo, j < hi) for lo, hi in q_blocks])
    scale = jnp.where(is_q, jnp.float32(q_scale), jnp.float32(1.0))
    o_ref[...] = (acc * scale).astype(BF16)


def _ln_inproj(x2, g, b, w_bf16, *, tm, tn, q_col_ranges):
    m, d = x2.shape
    n = w_bf16.shape[1]
    q_blocks = tuple((lo // tn, hi // tn) for lo, hi in q_col_ranges)
    kern = functools.partial(_ln_inproj_kernel, q_scale=HEAD_DIM ** -0.5, q_blocks=q_blocks)
    return pl.pallas_call(
        kern,
        grid=(m // tm, n // tn),
        in_specs=[
            pl.BlockSpec((tm, d), lambda i, j: (i, 0)),
            pl.BlockSpec((1, d), lambda i, j: (0, 0)),
            pl.BlockSpec((1, d), lambda i, j: (0, 0)),
            pl.BlockSpec((d, tn), lambda i, j: (0, j)),
        ],
        out_specs=pl.BlockSpec((tm, tn), lambda i, j: (i, j)),
        out_shape=jax.ShapeDtypeStruct((m, n), BF16),
        scratch_shapes=[pltpu.VMEM((tm, d), BF16)],
        compiler_params=pltpu.CompilerParams(
            dimension_semantics=("arbitrary", "arbitrary"), vmem_limit_bytes=VMEM_LIMIT_BYTES),
        name="ln_inproj",
    )(x2, g, b, w_bf16)


def _na_kernel(q_ref, k_ref, v_ref, bias_ref, o_ref, *, rows, kh, unroll):
    band = kh * GRID_W

    def one_row(r):
        rs = jnp.clip(r - kh // 2, 0, rows - kh)
        q = q_ref[pl.ds(pl.multiple_of(r * GRID_W, GRID_W), GRID_W), :]
        kb = k_ref[pl.ds(pl.multiple_of(rs * GRID_W, GRID_W), band), :]
        vb = v_ref[pl.ds(pl.multiple_of(rs * GRID_W, GRID_W), band), :]
        s = lax.dot_general(q, kb, (((1,), (1,)), ((), ())), preferred_element_type=F32)
        s = s + bias_ref[0, r - rs]
        m = jnp.max(s, axis=-1, keepdims=True)
        p = jnp.exp(s - m)
        l = jnp.sum(p, axis=-1, keepdims=True)
        p = (p * (1.0 / l)).astype(BF16)
        o = jnp.dot(p, vb, preferred_element_type=F32)
        o_ref[pl.ds(pl.multiple_of(r * GRID_W, GRID_W), GRID_W), :] = o.astype(o_ref.dtype)

    def body(it, carry):
        for t in range(unroll):
            one_row(it * unroll + t)
        return carry

    lax.fori_loop(0, rows // unroll, body, 0)


def _na_bias_table(rpb, rows):
    kh = min(NA_WIN_ROWS, rows)
    kw = NA_WIN_COLS
    cols = jnp.arange(GRID_W)
    c_start = jnp.clip(cols - kw // 2, 0, GRID_W - kw)
    ck = jnp.arange(GRID_W)
    valid = (ck[None, :] >= c_start[:, None]) & (ck[None, :] < c_start[:, None] + kw)
    dc = jnp.clip(ck[None, :] - cols[:, None] + (NA_WIN_COLS - 1), 0, 2 * NA_WIN_COLS - 2)
    off = jnp.arange(kh)
    i = jnp.arange(kh)
    dr = i[None, :] - off[:, None] + (NA_WIN_ROWS - 1)
    t = rpb[:, dr][:, :, :, dc]
    t = jnp.where(valid[None, None, None], t, MASK_VALUE)
    t = jnp.transpose(t, (0, 1, 3, 2, 4))
    return t.reshape(rpb.shape[0], kh, GRID_W, kh * GRID_W).astype(F32)


def _neighbourhood_attention(proj, bias, *, batch, seq):
    rows = seq // GRID_W
    kh = min(NA_WIN_ROWS, rows)
    kern = functools.partial(_na_kernel, rows=rows, kh=kh, unroll=8)
    return pl.pallas_call(
        kern,
        grid=(batch, NA_HEADS),
        in_specs=[
            pl.BlockSpec((seq, HEAD_DIM), lambda b, h: (b, h)),
            pl.BlockSpec((seq, HEAD_DIM), lambda b, h: (b, NA_HEADS + h)),
            pl.BlockSpec((seq, HEAD_DIM), lambda b, h: (b, 2 * NA_HEADS + h)),
            pl.BlockSpec((1, kh, GRID_W, kh * GRID_W), lambda b, h: (h, 0, 0, 0)),
        ],
        out_specs=pl.BlockSpec((seq, HEAD_DIM), lambda b, h: (b, h)),
        out_shape=jax.ShapeDtypeStruct((batch * seq, NA_HEADS * HEAD_DIM), BF16),
        compiler_params=pltpu.CompilerParams(
            dimension_semantics=("arbitrary", "arbitrary"), vmem_limit_bytes=VMEM_LIMIT_BYTES),
        name="na_attention",
    )(proj, proj, proj, bias)


def _t5_bucket(rel):
    nb = REL_BUCKETS // 2
    max_exact = nb // 2
    ret = (rel > 0).astype(jnp.int32) * nb
    n = jnp.abs(rel)
    nf = jnp.maximum(n, 1).astype(jnp.float32)
    large = max_exact + (jnp.log(nf / max_exact) / math.log(REL_MAX_DIST / max_exact) * (nb - max_exact)).astype(jnp.int32)
    large = jnp.minimum(large, nb - 1)
    return ret + jnp.where(n < max_exact, n, large)


def _diff_bias_table(rel_table, tq):
    assert tq > REL_MAX_DIST
    ql = jnp.arange(tq, dtype=jnp.int32)
    j = jnp.arange(5 * tq, dtype=jnp.int32) - 2 * tq
    rel = j[None, :] - ql[:, None]
    bias = rel_table[_t5_bucket(rel)]
    return jnp.transpose(bias, (2, 0, 1)).astype(F32)


def _diff_kernel(q_ref, k_ref, v_ref, bias_ref, lq1_ref, lk1_ref, lq2_ref, lk2_ref, g_ref,
                 o_ref, s_ref, w_ref, *, tq, seq, rt, lambda_init):
    i = pl.program_id(2)
    nchunk = seq // tq

    lam = (jnp.exp(jnp.sum(lq1_ref[...] * lk1_ref[...], axis=-1, keepdims=True))
           - jnp.exp(jnp.sum(lq2_ref[...] * lk2_ref[...], axis=-1, keepdims=True))
           + lambda_init)

    for m in range(2):
        qm = q_ref[:, m * DIFF_QK_DIM:(m + 1) * DIFF_QK_DIM]
        km = k_ref[:, m * DIFF_QK_DIM:(m + 1) * DIFF_QK_DIM]
        s_ref[m] = lax.dot_general(qm, km, (((1,), (1,)), ((), ())), preferred_element_type=F32)

    offs = [pl.multiple_of((jnp.clip(c - i, -2, 2) + 2) * tq, tq) for c in range(nchunk)]

    def row_tile(t, carry):
        r0 = pl.multiple_of(t * rt, rt)
        rows = pl.ds(r0, rt)
        mx = [None, None]
        for c in range(nchunk):
            b = bias_ref[0, rows, pl.ds(offs[c], tq)]
            for m in range(2):
                sc = s_ref[m, rows, c * tq:(c + 1) * tq] + b
                s_ref[m, rows, c * tq:(c + 1) * tq] = sc
                mx[m] = sc if mx[m] is None else jnp.maximum(mx[m], sc)
        mrow = [jnp.max(mx[m], axis=-1, keepdims=True) for m in range(2)]
        ls = [None, None]
        for c in range(nchunk):
            for m in range(2):
                p = jnp.exp(s_ref[m, rows, c * tq:(c + 1) * tq] - mrow[m])
                s_ref[m, rows, c * tq:(c + 1) * tq] = p
                ls[m] = p if ls[m] is None else ls[m] + p
        l1 = jnp.sum(ls[0], axis=-1, keepdims=True)
        l2 = jnp.sum(ls[1], axis=-1, keepdims=True)
        r1 = 1.0 / l1
        r2 = lam / l2
        for c in range(nchunk):
            w = s_ref[0, rows, c * tq:(c + 1) * tq] * r1 - s_ref[1, rows, c * tq:(c + 1) * tq] * r2
            w_ref[rows, c * tq:(c + 1) * tq] = w.astype(BF16)
        return carry

    lax.fori_loop(0, tq // rt, row_tile, 0)

    acc = jnp.dot(w_ref[...], v_ref[...], preferred_element_type=F32)
    ms = jnp.mean(acc * acc, axis=-1, keepdims=True)
    out = acc * lax.rsqrt(ms + LN_EPS) * g_ref[...] * (1.0 - lambda_init)
    o_ref[...] = out.astype(o_ref.dtype)


def _differential_attention(proj, bias, lq1, lk1, lq2, lk2, subln_g, *, batch, seq, tq, col0, lambda_init):
    nq = seq // tq
    qk_w = 2 * DIFF_QK_DIM
    qb0 = col0 // qk_w
    kb0 = qb0 + DIFF_HEADS
    vb0 = kb0 + DIFF_HEADS
    kern = functools.partial(_diff_kernel, tq=tq, seq=seq, rt=32, lambda_init=lambda_init)
    vec = lambda n: pl.BlockSpec((1, n), lambda b, h, i: (0, 0))
    return pl.pallas_call(
        kern,
        grid=(batch, DIFF_HEADS, nq),
        in_specs=[
            pl.BlockSpec((tq, qk_w), lambda b, h, i: (b * nq + i, qb0 + h)),
            pl.BlockSpec((seq, qk_w), lambda b, h, i: (b, kb0 + h)),
            pl.BlockSpec((seq, DIFF_V_DIM), lambda b, h, i: (b, vb0 + h)),
            pl.BlockSpec((1, tq, 5 * tq), lambda b, h, i: (h, 0, 0)),
            vec(DIFF_QK_DIM), vec(DIFF_QK_DIM), vec(DIFF_QK_DIM), vec(DIFF_QK_DIM),
            vec(DIFF_V_DIM),
        ],
        out_specs=pl.BlockSpec((tq, DIFF_V_DIM), lambda b, h, i: (b * nq + i, h)),
        out_shape=jax.ShapeDtypeStruct((batch * seq, DIFF_HEADS * DIFF_V_DIM), BF16),
        scratch_shapes=[pltpu.VMEM((2, tq, seq), F32), pltpu.VMEM((tq, seq), BF16)],
        compiler_params=pltpu.CompilerParams(
            dimension_semantics=("arbitrary", "arbitrary", "arbitrary"), vmem_limit_bytes=VMEM_LIMIT_BYTES),
        name="diff_attention",
    )(proj, proj, proj, bias, lq1, lk1, lq2, lk2, subln_g)


def _outproj_kernel(na_ref, df_ref, wa_ref, wd_ref, x_ref, gin_ref, bin_ref, g1_ref, b1_ref,
                    of_ref, ob_ref, acc_ref, *, tn):
    j = pl.program_id(1)
    nj = pl.num_programs(1)
    acc = jnp.dot(na_ref[...], wa_ref[...], preferred_element_type=F32)
    acc = acc + jnp.dot(df_ref[...], wd_ref[...], preferred_element_type=F32)
    acc_ref[:, pl.ds(pl.multiple_of(j * tn, tn), tn)] = acc

    @pl.when(j == nj - 1)
    def _():
        h0 = _layer_norm_rows(x_ref[...], gin_ref[...], bin_ref[...])
        y = DEEPNORM_ALPHA * h0 + acc_ref[...]
        h1 = _layer_norm_rows(y, g1_ref[...], b1_ref[...])
        of_ref[...] = h1
        ob_ref[...] = h1.astype(BF16)


def _outproj_ln(na_out, diff_out, w_na, w_diff, x2, gin, bin_, g1, b1, *, tm, tn):
    m, d = x2.shape
    ka = na_out.shape[1]
    kd = diff_out.shape[1]
    kern = functools.partial(_outproj_kernel, tn=tn)
    vec = pl.BlockSpec((1, d), lambda i, j: (0, 0))
    return pl.pallas_call(
        kern,
        grid=(m // tm, d // tn),
        in_specs=[
            pl.BlockSpec((tm, ka), lambda i, j: (i, 0)),
            pl.BlockSpec((tm, kd), lambda i, j: (i, 0)),
            pl.BlockSpec((ka, tn), lambda i, j: (0, j)),
            pl.BlockSpec((kd, tn), lambda i, j: (0, j)),
            pl.BlockSpec((tm, d), lambda i, j: (i, 0)),
            vec, vec, vec, vec,
        ],
        out_specs=[pl.BlockSpec((tm, d), lambda i, j: (i, 0)),
                   pl.BlockSpec((tm, d), lambda i, j: (i, 0))],
        out_shape=[jax.ShapeDtypeStruct((m, d), F32), jax.ShapeDtypeStruct((m, d), BF16)],
        scratch_shapes=[pltpu.VMEM((tm, d), F32)],
        compiler_params=pltpu.CompilerParams(
            dimension_semantics=("arbitrary", "arbitrary"), vmem_limit_bytes=VMEM_LIMIT_BYTES),
        name="outproj_ln",
    )(na_out, diff_out, w_na, w_diff, x2, gin, bin_, g1, b1)


def _ffn_kernel(hb_ref, hf_ref, wg_ref, wu_ref, wd_ref, g2_ref, b2_ref, o_ref, acc_ref):
    j = pl.program_id(1)
    nj = pl.num_programs(1)
    h = hb_ref[...]
    g = jnp.dot(h, wg_ref[...], preferred_element_type=F32)
    u = jnp.dot(h, wu_ref[...], preferred_element_type=F32)
    a = (g * jax.nn.sigmoid(g) * u).astype(BF16)
    part = jnp.dot(a, wd_ref[...], preferred_element_type=F32)

    @pl.when(j == 0)
    def _():
        acc_ref[...] = part

    @pl.when(j > 0)
    def _():
        acc_ref[...] += part

    @pl.when(j == nj - 1)
    def _():
        y = DEEPNORM_ALPHA * hf_ref[...] + acc_ref[...]
        o_ref[...] = _layer_norm_rows(y, g2_ref[...], b2_ref[...])


def _ffn_ln(h1b, h1f, wg, wu, wd, g2, b2, *, tm, tf):
    m, d = h1f.shape
    f = wg.shape[1]
    vec = pl.BlockSpec((1, d), lambda i, j: (0, 0))
    return pl.pallas_call(
        _ffn_kernel,
        grid=(m // tm, f // tf),
        in_specs=[
            pl.BlockSpec((tm, d), lambda i, j: (i, 0)),
            pl.BlockSpec((tm, d), lambda i, j: (i, 0)),
            pl.BlockSpec((d, tf), lambda i, j: (0, j)),
            pl.BlockSpec((d, tf), lambda i, j: (0, j)),
            pl.BlockSpec((tf, d), lambda i, j: (j, 0)),
            vec, vec,
        ],
        out_specs=pl.BlockSpec((tm, d), lambda i, j: (i, 0)),
        out_shape=jax.ShapeDtypeStruct((m, d), F32),
        scratch_shapes=[pltpu.VMEM((tm, d), F32)],
        compiler_params=pltpu.CompilerParams(
            dimension_semantics=("arbitrary", "arbitrary"), vmem_limit_bytes=VMEM_LIMIT_BYTES),
        name="ffn_ln",
    )(h1b, h1f, wg, wu, wd, g2, b2)


def kernel(x, ln_in_g, ln_in_b, w_in, na_rpb, lambda_q1, lambda_k1, lambda_q2, lambda_k2, diff_subln_g, rel_bias_table, w_out, ln1_g, ln1_b, w_gate, w_up, w_down, ln2_g, ln2_b):
    batch, seq, d = x.shape
    assert w_in.shape[0] == DEPTH == 1
    layer = 0
    na_w = NA_HEADS * HEAD_DIM
    diff_qk_w = DIFF_HEADS * 2 * DIFF_QK_DIM
    rows = seq // GRID_W

    x2 = x.reshape(batch * seq, d)
    row = lambda v: v.reshape(1, -1).astype(F32)

    q_cols = ((0, na_w), (3 * na_w, 3 * na_w + diff_qk_w))
    proj = _ln_inproj(x2, row(ln_in_g), row(ln_in_b), w_in[layer].astype(BF16),
                      tm=1024, tn=512, q_col_ranges=q_cols)

    na_out = _neighbourhood_attention(proj, _na_bias_table(na_rpb[layer], rows), batch=batch, seq=seq)

    tq = 256
    lambda_init = 0.8 - 0.6 * math.exp(-0.3 * layer)
    diff_out = _differential_attention(
        proj, _diff_bias_table(rel_bias_table, tq),
        row(lambda_q1[layer]), row(lambda_k1[layer]), row(lambda_q2[layer]), row(lambda_k2[layer]),
        row(diff_subln_g[layer]), batch=batch, seq=seq, tq=tq, col0=3 * na_w, lambda_init=lambda_init)

    w_o = w_out[layer].astype(BF16)
    h1f, h1b = _outproj_ln(na_out, diff_out, w_o[:na_w], w_o[na_w:], x2, row(ln_in_g), row(ln_in_b),
                           row(ln1_g[layer]), row(ln1_b[layer]), tm=512, tn=512)

    out = _ffn_ln(h1b, h1f, w_gate[layer].astype(BF16), w_up[layer].astype(BF16), w_down[layer].astype(BF16),
                  row(ln2_g[layer]), row(ln2_b[layer]), tm=512, tf=512)
    return out.reshape(batch, seq, d)
```

```python
import functools
import math

import jax
import jax.numpy as jnp
from jax import lax
from jax.experimental import pallas as pl
from jax.experimental.pallas import tpu as pltpu

F32 = jnp.float32
BF16 = jnp.bfloat16

GRID_W = 64
HEAD_DIM = 128
NA_HEADS = 8
NA_WIN_ROWS = 8
NA_WIN_COLS = 16
DIFF_HEADS = 4
DIFF_QK_DIM = HEAD_DIM
DIFF_V_DIM = 2 * HEAD_DIM
REL_BUCKETS = 32
REL_MAX_DIST = 128
LN_EPS = 1e-5
DEPTH = 1
DEEPNORM_ALPHA = (2.0 * DEPTH) ** 0.25
MASK_VALUE = -1e30
LOG2E = math.log2(math.e)
LANES = 128

VMEM_LIMIT_BYTES = 56 * 1024 * 1024


def _layer_norm_rows(x, g, b):
    mu = jnp.mean(x, axis=-1, keepdims=True)
    xc = x - mu
    var = jnp.mean(xc * xc, axis=-1, keepdims=True)
    return xc * lax.rsqrt(var + LN_EPS) * g + b


def _toeplitz(w_pad, nrows):
    p = w_pad.shape[-1]
    flat = jnp.tile(w_pad, (1,) * (w_pad.ndim - 1) + (nrows,))[..., :nrows * (p - 1)]
    return flat.reshape(w_pad.shape[:-1] + (nrows, p - 1))


def _ln_inproj_kernel(x_ref, g_ref, b_ref, w_ref, o_ref, h_ref, *, q_scale, q_blocks):
    j = pl.program_id(1)

    @pl.when(j == 0)
    def _():
        h_ref[...] = _layer_norm_rows(x_ref[...], g_ref[...], b_ref[...]).astype(BF16)

    acc = jnp.dot(h_ref[...], w_ref[...], preferred_element_type=F32)
    is_q = functools.reduce(jnp.logical_or, [jnp.logical_and(j >= lo, j < hi) for lo, hi in q_blocks])
    scale = jnp.where(is_q, jnp.float32(q_scale), jnp.float32(1.0))
    o_ref[...] = (acc * scale).astype(BF16)


def _ln_inproj(x2, g, b, w_bf16, *, tm, tn, q_col_ranges, q_scale):
    m, d = x2.shape
    n = w_bf16.shape[1]
    q_blocks = tuple((lo // tn, hi // tn) for lo, hi in q_col_ranges)
    kern = functools.partial(_ln_inproj_kernel, q_scale=q_scale, q_blocks=q_blocks)
    return pl.pallas_call(
        kern,
        grid=(m // tm, n // tn),
        in_specs=[
            pl.BlockSpec((tm, d), lambda i, j: (i, 0)),
            pl.BlockSpec((1, d), lambda i, j: (0, 0)),
            pl.BlockSpec((1, d), lambda i, j: (0, 0)),
            pl.BlockSpec((d, tn), lambda i, j: (0, j)),
        ],
        out_specs=pl.BlockSpec((tm, tn), lambda i, j: (i, j)),
        out_shape=jax.ShapeDtypeStruct((m, n), BF16),
        scratch_shapes=[pltpu.VMEM((tm, d), BF16)],
        compiler_params=pltpu.CompilerParams(
            dimension_semantics=("arbitrary", "arbitrary"), vmem_limit_bytes=VMEM_LIMIT_BYTES),
        name="ln_inproj",
    )(x2, g, b, w_bf16)


NA_BLOCK_ROWS = 8
NA_WIN_PAIRS = NA_WIN_ROWS // 2 + 1
NA_BAND_PAIRS = 8
NA_VARIANTS = 10


def _na_variant_rows(var):
    if var <= 4:
        return var, 0
    if var == 5:
        return 4, 1
    return var - 2, 2


def _na_kernel(q_ref, k_ref, v_ref, tt_ref, o_ref, bb_ref, s_ref, p_ref, *, rows):
    kh = NA_WIN_ROWS
    pair = 2 * GRID_W
    blk_q = NA_BLOCK_ROWS * GRID_W
    band = NA_BAND_PAIRS * pair
    win = NA_WIN_PAIRS * pair
    n_pairs = rows // 2

    @pl.when(pl.program_id(1) == 0)
    def _():
        lane = lax.broadcasted_iota(jnp.int32, (GRID_W, pair), 1)
        for var in range(NA_VARIANTS):
            idx, e = _na_variant_rows(var)
            for p in range(NA_WIN_PAIRS):
                halves = []
                for half in range(2):
                    i = 2 * p + half - e
                    dr = i - idx + (NA_WIN_ROWS - 1) if 0 <= i < kh else 2 * NA_WIN_ROWS - 1
                    halves.append(tt_ref[0, dr])
                bb_ref[var, :, p * pair:(p + 1) * pair] = jnp.where(lane < GRID_W, halves[0], halves[1])

    def block(blk, carry):
        r0 = blk * NA_BLOCK_ROWS
        rs0 = jnp.clip(r0 - kh // 2, 0, rows - kh)
        bp = jnp.minimum(rs0 // 2, n_pairs - NA_BAND_PAIRS)
        q0 = pl.multiple_of(r0 * GRID_W, blk_q)
        k0 = pl.multiple_of(bp * pair, pair)
        s_ref[...] = lax.dot_general(q_ref[pl.ds(q0, blk_q), :], k_ref[pl.ds(k0, band), :],
                                     (((1,), (1,)), ((), ())), preferred_element_type=F32)
        p_ref[...] = jnp.zeros(p_ref.shape, p_ref.dtype)
        for j in range(NA_BLOCK_ROWS):
            r = r0 + j
            rs = jnp.clip(r - kh // 2, 0, rows - kh)
            ps = jnp.minimum(rs // 2, n_pairs - NA_WIN_PAIRS)
            var = (r - rs) + (rs - 2 * ps)
            lanes = pl.ds(pl.multiple_of((ps - bp) * pair, pair), win)
            qrows = slice(j * GRID_W, (j + 1) * GRID_W)
            s = s_ref[qrows, lanes] + bb_ref[var]
            m = jnp.max(s, axis=-1, keepdims=True)
            p = jnp.exp2(s - m)
            l = jnp.sum(p, axis=-1, keepdims=True)
            p_ref[qrows, lanes] = (p * (1.0 / l)).astype(BF16)
        o = jnp.dot(p_ref[...], v_ref[pl.ds(k0, band), :], preferred_element_type=F32)
        o_ref[pl.ds(q0, blk_q), :] = o.astype(o_ref.dtype)
        return carry

    lax.fori_loop(0, rows // NA_BLOCK_ROWS, block, 0)


def _na_bias_tiles(rpb):
    h, ndr, ndc = rpb.shape
    assert ndr == 2 * NA_WIN_ROWS - 1 and ndc == 2 * NA_WIN_COLS - 1
    shift = GRID_W - NA_WIN_COLS
    w_pad = jnp.zeros((h, ndr, 2 * GRID_W), F32).at[:, :, shift:shift + ndc].set(rpb)
    tiles = _toeplitz(w_pad, GRID_W)[..., GRID_W - 1:2 * GRID_W - 1]
    cols = jnp.arange(GRID_W)
    c_start = jnp.clip(cols - NA_WIN_COLS // 2, 0, GRID_W - NA_WIN_COLS)
    valid = (cols[None, :] >= c_start[:, None]) & (cols[None, :] < c_start[:, None] + NA_WIN_COLS)
    tiles = jnp.where(valid[None, None], tiles, MASK_VALUE)
    tiles = jnp.concatenate([tiles, jnp.full((h, 1, GRID_W, GRID_W), MASK_VALUE, F32)], axis=1)
    return jnp.concatenate([tiles, tiles], axis=-1)


def _neighbourhood_attention(proj, tiles, *, batch, seq):
    rows = seq // GRID_W
    assert min(NA_WIN_ROWS, rows) == NA_WIN_ROWS and rows % NA_BLOCK_ROWS == 0 and rows // 2 >= NA_BAND_PAIRS
    kern = functools.partial(_na_kernel, rows=rows)
    pair = 2 * GRID_W
    return pl.pallas_call(
        kern,
        grid=(NA_HEADS, batch),
        in_specs=[
            pl.BlockSpec((seq, HEAD_DIM), lambda h, b: (b, h)),
            pl.BlockSpec((seq, HEAD_DIM), lambda h, b: (b, NA_HEADS + h)),
            pl.BlockSpec((seq, HEAD_DIM), lambda h, b: (b, 2 * NA_HEADS + h)),
            pl.BlockSpec((1, 2 * NA_WIN_ROWS, GRID_W, pair), lambda h, b: (h, 0, 0, 0)),
        ],
        out_specs=pl.BlockSpec((seq, HEAD_DIM), lambda h, b: (b, h)),
        out_shape=jax.ShapeDtypeStruct((batch * seq, NA_HEADS * HEAD_DIM), BF16),
        scratch_shapes=[
            pltpu.VMEM((NA_VARIANTS, GRID_W, NA_WIN_PAIRS * pair), F32),
            pltpu.VMEM((NA_BLOCK_ROWS * GRID_W, NA_BAND_PAIRS * pair), F32),
            pltpu.VMEM((NA_BLOCK_ROWS * GRID_W, NA_BAND_PAIRS * pair), BF16),
        ],
        compiler_params=pltpu.CompilerParams(
            dimension_semantics=("arbitrary", "arbitrary"), vmem_limit_bytes=VMEM_LIMIT_BYTES),
        name="na_attention",
    )(proj, proj, proj, tiles)


def _t5_bucket(rel):
    nb = REL_BUCKETS // 2
    max_exact = nb // 2
    ret = (rel > 0).astype(jnp.int32) * nb
    n = jnp.abs(rel)
    nf = jnp.maximum(n, 1).astype(jnp.float32)
    large = max_exact + (jnp.log(nf / max_exact) / math.log(REL_MAX_DIST / max_exact) * (nb - max_exact)).astype(jnp.int32)
    large = jnp.minimum(large, nb - 1)
    return ret + jnp.where(n < max_exact, n, large)


def _diff_bias_table(rel_table, tq):
    assert tq >= REL_MAX_DIST
    n = 6 * tq
    rel = jnp.arange(n, dtype=jnp.int32) - (3 * tq - 1)
    bucket = _t5_bucket(rel)
    onehot = bucket[:, None] == jnp.arange(REL_BUCKETS, dtype=jnp.int32)[None, :]
    vec = jnp.sum(jnp.where(onehot[None], rel_table.T[:, None, :], 0.0), axis=-1)
    return _toeplitz(vec.astype(F32), tq)[..., tq - 1:tq - 1 + 5 * tq]


def _diff_kernel(q_ref, k_ref, v_ref, bias_ref, lq1_ref, lk1_ref, lq2_ref, lk2_ref, g_ref,
                 o_ref, s_ref, p_ref, *, tq, seq, grp, lambda_init):
    i = pl.program_id(2)
    nchunk = seq // tq
    ngrp = seq // grp
    cpg = grp // tq

    lam = (jnp.exp(jnp.sum(lq1_ref[...] * lk1_ref[...], axis=-1, keepdims=True))
           - jnp.exp(jnp.sum(lq2_ref[...] * lk2_ref[...], axis=-1, keepdims=True))
           + lambda_init)

    panel = [pl.multiple_of((jnp.clip(c - i, -2, 2) + 2) * tq, tq) for c in range(nchunk)]

    def fold_lanes(acc, x, op):
        for hh in range(x.shape[1] // LANES):
            part = x[:, hh * LANES:(hh + 1) * LANES]
            acc = part if acc is None else op(acc, part)
        return acc

    mrow = []
    for m in range(2):
        qm = q_ref[:, m * DIFF_QK_DIM:(m + 1) * DIFF_QK_DIM]
        mx = None
        for g in range(ngrp):
            kg = k_ref[g * grp:(g + 1) * grp, m * DIFF_QK_DIM:(m + 1) * DIFF_QK_DIM]
            s = lax.dot_general(qm, kg, (((1,), (1,)), ((), ())), preferred_element_type=F32)
            for cc in range(cpg):
                c = g * cpg + cc
                sc = s[:, cc * tq:(cc + 1) * tq] + bias_ref[0, :, pl.ds(panel[c], tq)]
                s_ref[m, :, c * tq:(c + 1) * tq] = sc
                mx = fold_lanes(mx, sc, jnp.maximum)
        mrow.append(jnp.max(mx, axis=-1, keepdims=True))

    acc = []
    lrow = []
    for m in range(2):
        ls = None
        am = None
        for g in range(ngrp):
            for cc in range(cpg):
                c = g * cpg + cc
                p = jnp.exp2(s_ref[m, :, c * tq:(c + 1) * tq] - mrow[m])
                ls = fold_lanes(ls, p, jnp.add)
                p_ref[m, :, c * tq:(c + 1) * tq] = p.astype(BF16)
            part = jnp.dot(p_ref[m, :, g * grp:(g + 1) * grp], v_ref[g * grp:(g + 1) * grp, :],
                           preferred_element_type=F32)
            am = part if am is None else am + part
        acc.append(am)
        lrow.append(jnp.sum(ls, axis=-1, keepdims=True))

    out = acc[0] * (1.0 / lrow[0]) - acc[1] * (lam / lrow[1])
    ms = jnp.mean(out * out, axis=-1, keepdims=True)
    out = out * lax.rsqrt(ms + LN_EPS) * g_ref[...] * (1.0 - lambda_init)
    o_ref[...] = out.astype(o_ref.dtype)


def _differential_attention(proj, bias, lq1, lk1, lq2, lk2, subln_g, *, batch, seq, tq, col0, lambda_init):
    nq = seq // tq
    qk_w = 2 * DIFF_QK_DIM
    qb0 = col0 // qk_w
    kb0 = qb0 + DIFF_HEADS
    vb0 = kb0 + DIFF_HEADS
    kern = functools.partial(_diff_kernel, tq=tq, seq=seq, grp=4 * tq, lambda_init=lambda_init)
    vec = lambda n: pl.BlockSpec((1, n), lambda b, h, i: (0, 0))
    return pl.pallas_call(
        kern,
        grid=(batch, DIFF_HEADS, nq),
        in_specs=[
            pl.BlockSpec((tq, qk_w), lambda b, h, i: (b * nq + i, qb0 + h)),
            pl.BlockSpec((seq, qk_w), lambda b, h, i: (b, kb0 + h)),
            pl.BlockSpec((seq, DIFF_V_DIM), lambda b, h, i: (b, vb0 + h)),
            pl.BlockSpec((1, tq, 5 * tq), lambda b, h, i: (h, 0, 0)),
            vec(DIFF_QK_DIM), vec(DIFF_QK_DIM), vec(DIFF_QK_DIM), vec(DIFF_QK_DIM),
            vec(DIFF_V_DIM),
        ],
        out_specs=pl.BlockSpec((tq, DIFF_V_DIM), lambda b, h, i: (b * nq + i, h)),
        out_shape=jax.ShapeDtypeStruct((batch * seq, DIFF_HEADS * DIFF_V_DIM), BF16),
        scratch_shapes=[pltpu.VMEM((2, tq, seq), F32), pltpu.VMEM((2, tq, seq), BF16)],
        compiler_params=pltpu.CompilerParams(
            dimension_semantics=("arbitrary", "arbitrary", "arbitrary"), vmem_limit_bytes=VMEM_LIMIT_BYTES),
        name="diff_attention",
    )(proj, proj, proj, bias, lq1, lk1, lq2, lk2, subln_g)


def _outproj_kernel(na_ref, df_ref, wa_ref, wd_ref, x_ref, gin_ref, bin_ref, g1_ref, b1_ref,
                    of_ref, ob_ref, acc_ref, *, tn):
    j = pl.program_id(1)
    nj = pl.num_programs(1)
    acc = jnp.dot(na_ref[...], wa_ref[...], preferred_element_type=F32)
    acc = acc + jnp.dot(df_ref[...], wd_ref[...], preferred_element_type=F32)
    acc_ref[:, pl.ds(pl.multiple_of(j * tn, tn), tn)] = acc

    @pl.when(j == nj - 1)
    def _():
        h0 = _layer_norm_rows(x_ref[...], gin_ref[...], bin_ref[...])
        y = DEEPNORM_ALPHA * h0 + acc_ref[...]
        h1 = _layer_norm_rows(y, g1_ref[...], b1_ref[...])
        of_ref[...] = h1
        ob_ref[...] = h1.astype(BF16)


def _outproj_ln(na_out, diff_out, w_out, x2, gin, bin_, g1, b1, *, tm, tn):
    m, d = x2.shape
    ka = na_out.shape[1]
    kd = diff_out.shape[1]
    assert ka == kd and w_out.shape == (ka + kd, d)
    kern = functools.partial(_outproj_kernel, tn=tn)
    vec = pl.BlockSpec((1, d), lambda i, j: (0, 0))
    return pl.pallas_call(
        kern,
        grid=(m // tm, d // tn),
        in_specs=[
            pl.BlockSpec((tm, ka), lambda i, j: (i, 0)),
            pl.BlockSpec((tm, kd), lambda i, j: (i, 0)),
            pl.BlockSpec((ka, tn), lambda i, j: (0, j)),
            pl.BlockSpec((kd, tn), lambda i, j: (1, j)),
            pl.BlockSpec((tm, d), lambda i, j: (i, 0)),
            vec, vec, vec, vec,
        ],
        out_specs=[pl.BlockSpec((tm, d), lambda i, j: (i, 0)),
                   pl.BlockSpec((tm, d), lambda i, j: (i, 0))],
        out_shape=[jax.ShapeDtypeStruct((m, d), F32), jax.ShapeDtypeStruct((m, d), BF16)],
        scratch_shapes=[pltpu.VMEM((tm, d), F32)],
        compiler_params=pltpu.CompilerParams(
            dimension_semantics=("arbitrary", "arbitrary"), vmem_limit_bytes=VMEM_LIMIT_BYTES),
        name="outproj_ln",
    )(na_out, diff_out, w_out, w_out, x2, gin, bin_, g1, b1)


def _ffn_kernel(hb_ref, hf_ref, wg_ref, wu_ref, wd_ref, g2_ref, b2_ref, o_ref, acc_ref):
    j = pl.program_id(1)
    nj = pl.num_programs(1)
    h = hb_ref[...]
    g = jnp.dot(h, wg_ref[...], preferred_element_type=F32)
    u = jnp.dot(h, wu_ref[...], preferred_element_type=F32)
    a = (g * jax.nn.sigmoid(g) * u).astype(BF16)
    part = jnp.dot(a, wd_ref[...], preferred_element_type=F32)

    @pl.when(j == 0)
    def _():
        acc_ref[...] = part

    @pl.when(j > 0)
    def _():
        acc_ref[...] += part

    @pl.when(j == nj - 1)
    def _():
        y = DEEPNORM_ALPHA * hf_ref[...] + acc_ref[...]
        o_ref[...] = _layer_norm_rows(y, g2_ref[...], b2_ref[...])


def _ffn_ln(h1b, h1f, wg, wu, wd, g2, b2, *, tm, tf):
    m, d = h1f.shape
    f = wg.shape[1]
    vec = pl.BlockSpec((1, d), lambda i, j: (0, 0))
    return pl.pallas_call(
        _ffn_kernel,
        grid=(m // tm, f // tf),
        in_specs=[
            pl.BlockSpec((tm, d), lambda i, j: (i, 0)),
            pl.BlockSpec((tm, d), lambda i, j: (i, 0)),
            pl.BlockSpec((d, tf), lambda i, j: (0, j)),
            pl.BlockSpec((d, tf), lambda i, j: (0, j)),
            pl.BlockSpec((tf, d), lambda i, j: (j, 0)),
            vec, vec,
        ],
        out_specs=pl.BlockSpec((tm, d), lambda i, j: (i, 0)),
        out_shape=jax.ShapeDtypeStruct((m, d), F32),
        scratch_shapes=[pltpu.VMEM((tm, d), F32)],
        compiler_params=pltpu.CompilerParams(
            dimension_semantics=("arbitrary", "arbitrary"), vmem_limit_bytes=VMEM_LIMIT_BYTES),
        name="ffn_ln",
    )(h1b, h1f, wg, wu, wd, g2, b2)


def kernel(x, ln_in_g, ln_in_b, w_in, na_rpb, lambda_q1, lambda_k1, lambda_q2, lambda_k2, diff_subln_g, rel_bias_table, w_out, ln1_g, ln1_b, w_gate, w_up, w_down, ln2_g, ln2_b):
    batch, seq, d = x.shape
    assert w_in.shape[0] == DEPTH == 1
    layer = 0
    na_w = NA_HEADS * HEAD_DIM
    diff_qk_w = DIFF_HEADS * 2 * DIFF_QK_DIM

    x2 = x.reshape(batch * seq, d)
    row = lambda v: v.reshape(1, -1).astype(F32)

    q_cols = ((0, na_w), (3 * na_w, 3 * na_w + diff_qk_w))
    proj = _ln_inproj(x2, row(ln_in_g), row(ln_in_b), w_in[layer].astype(BF16),
                      tm=1024, tn=512, q_col_ranges=q_cols, q_scale=HEAD_DIM ** -0.5 * LOG2E)

    na_out = _neighbourhood_attention(proj, _na_bias_tiles(na_rpb[layer].astype(F32) * LOG2E), batch=batch, seq=seq)

    tq = 256
    lambda_init = 0.8 - 0.6 * math.exp(-0.3 * layer)
    diff_out = _differential_attention(
        proj, _diff_bias_table(rel_bias_table.astype(F32) * LOG2E, tq),
        row(lambda_q1[layer]), row(lambda_k1[layer]), row(lambda_q2[layer]), row(lambda_k2[layer]),
        row(diff_subln_g[layer]), batch=batch, seq=seq, tq=tq, col0=3 * na_w, lambda_init=lambda_init)

    h1f, h1b = _outproj_ln(na_out, diff_out, w_out[layer].astype(BF16), x2, row(ln_in_g), row(ln_in_b),
                           row(ln1_g[layer]), row(ln1_b[layer]), tm=512, tn=512)

    out = _ffn_ln(h1b, h1f, w_gate[layer].astype(BF16), w_up[layer].astype(BF16), w_down[layer].astype(BF16),
                  row(ln2_g[layer]), row(ln2_b[layer]), tm=512, tf=512)
    return out.reshape(batch, seq, d)
```

```python
import functools
import math

import jax
import jax.numpy as jnp
from jax import lax
from jax.experimental import pallas as pl
from jax.experimental.pallas import tpu as pltpu

F32 = jnp.float32
BF16 = jnp.bfloat16

GRID_W = 64
HEAD_DIM = 128
NA_HEADS = 8
NA_WIN_ROWS = 8
NA_WIN_COLS = 16
DIFF_HEADS = 4
DIFF_QK_DIM = HEAD_DIM
DIFF_V_DIM = 2 * HEAD_DIM
REL_BUCKETS = 32
REL_MAX_DIST = 128
LN_EPS = 1e-5
DEPTH = 1
DEEPNORM_ALPHA = (2.0 * DEPTH) ** 0.25
MASK_VALUE = -1e30
LOG2E = math.log2(math.e)
LANES = 128

VMEM_LIMIT_BYTES = 56 * 1024 * 1024


def _layer_norm_rows(x, g, b):
    mu = jnp.mean(x, axis=-1, keepdims=True)
    xc = x - mu
    var = jnp.mean(xc * xc, axis=-1, keepdims=True)
    return xc * lax.rsqrt(var + LN_EPS) * g + b


def _toeplitz(w_pad, nrows):
    p = w_pad.shape[-1]
    flat = jnp.tile(w_pad, (1,) * (w_pad.ndim - 1) + (nrows,))[..., :nrows * (p - 1)]
    return flat.reshape(w_pad.shape[:-1] + (nrows, p - 1))


def _ln_inproj_kernel(x_ref, g_ref, b_ref, w_ref, o_ref, h_ref, *, q_scale, q_blocks):
    j = pl.program_id(1)

    @pl.when(j == 0)
    def _():
        h_ref[...] = _layer_norm_rows(x_ref[...], g_ref[...], b_ref[...]).astype(BF16)

    acc = jnp.dot(h_ref[...], w_ref[...], preferred_element_type=F32)
    is_q = functools.reduce(jnp.logical_or, [jnp.logical_and(j >= lo, j < hi) for lo, hi in q_blocks])
    scale = jnp.where(is_q, jnp.float32(q_scale), jnp.float32(1.0))
    o_ref[...] = (acc * scale).astype(BF16)


def _ln_inproj(x2, g, b, w_bf16, *, tm, tn, q_col_ranges, q_scale):
    m, d = x2.shape
    n = w_bf16.shape[1]
    q_blocks = tuple((lo // tn, hi // tn) for lo, hi in q_col_ranges)
    kern = functools.partial(_ln_inproj_kernel, q_scale=q_scale, q_blocks=q_blocks)
    return pl.pallas_call(
        kern,
        grid=(m // tm, n // tn),
        in_specs=[
            pl.BlockSpec((tm, d), lambda i, j: (i, 0)),
            pl.BlockSpec((1, d), lambda i, j: (0, 0)),
            pl.BlockSpec((1, d), lambda i, j: (0, 0)),
            pl.BlockSpec((d, tn), lambda i, j: (0, j)),
        ],
        out_specs=pl.BlockSpec((tm, tn), lambda i, j: (i, j)),
        out_shape=jax.ShapeDtypeStruct((m, n), BF16),
        scratch_shapes=[pltpu.VMEM((tm, d), BF16)],
        compiler_params=pltpu.CompilerParams(
            dimension_semantics=("arbitrary", "arbitrary"), vmem_limit_bytes=VMEM_LIMIT_BYTES),
        name="ln_inproj",
    )(x2, g, b, w_bf16)


NA_BLOCK_ROWS = 8
NA_WIN_PAIRS = NA_WIN_ROWS // 2 + 1
NA_BAND_PAIRS = 8
NA_VARIANTS = 10


def _na_variant_rows(var):
    if var <= 4:
        return var, 0
    if var == 5:
        return 4, 1
    return var - 2, 2


def _na_kernel(q_ref, k_ref, v_ref, tt_ref, o_ref, bb_ref, s_ref, p_ref, *, rows):
    kh = NA_WIN_ROWS
    pair = 2 * GRID_W
    blk_q = NA_BLOCK_ROWS * GRID_W
    band = NA_BAND_PAIRS * pair
    win = NA_WIN_PAIRS * pair
    n_pairs = rows // 2

    @pl.when(pl.program_id(1) == 0)
    def _():
        lane = lax.broadcasted_iota(jnp.int32, (GRID_W, pair), 1)
        for var in range(NA_VARIANTS):
            idx, e = _na_variant_rows(var)
            for p in range(NA_WIN_PAIRS):
                halves = []
                for half in range(2):
                    i = 2 * p + half - e
                    dr = i - idx + (NA_WIN_ROWS - 1) if 0 <= i < kh else 2 * NA_WIN_ROWS - 1
                    halves.append(tt_ref[0, dr])
                bb_ref[var, :, p * pair:(p + 1) * pair] = jnp.where(lane < GRID_W, halves[0], halves[1])

    clip = lambda v, lo, hi: max(lo, min(v, hi))
    for blk in range(rows // NA_BLOCK_ROWS):
        slot = blk % 2
        r0 = blk * NA_BLOCK_ROWS
        bp = min(clip(r0 - kh // 2, 0, rows - kh) // 2, n_pairs - NA_BAND_PAIRS)
        q0 = r0 * GRID_W
        k0 = bp * pair
        s_ref[slot] = lax.dot_general(q_ref[q0:q0 + blk_q, :], k_ref[k0:k0 + band, :],
                                      (((1,), (1,)), ((), ())), preferred_element_type=F32)
        p_ref[slot] = jnp.zeros(p_ref.shape[1:], p_ref.dtype)
        for j in range(NA_BLOCK_ROWS):
            r = r0 + j
            rs = clip(r - kh // 2, 0, rows - kh)
            ps = min(rs // 2, n_pairs - NA_WIN_PAIRS)
            var = r - 2 * ps
            l0 = (ps - bp) * pair
            qrows = slice(j * GRID_W, (j + 1) * GRID_W)
            s = s_ref[slot, qrows, l0:l0 + win] + bb_ref[var]
            m = jnp.max(s, axis=-1, keepdims=True)
            p = jnp.exp2(s - m)
            l = jnp.sum(p, axis=-1, keepdims=True)
            p_ref[slot, qrows, l0:l0 + win] = (p * (1.0 / l)).astype(BF16)
        o = jnp.dot(p_ref[slot], v_ref[k0:k0 + band, :], preferred_element_type=F32)
        o_ref[q0:q0 + blk_q, :] = o.astype(o_ref.dtype)


def _na_bias_tiles(rpb):
    h, ndr, ndc = rpb.shape
    assert ndr == 2 * NA_WIN_ROWS - 1 and ndc == 2 * NA_WIN_COLS - 1
    shift = GRID_W - NA_WIN_COLS
    w_pad = jnp.zeros((h, ndr, 2 * GRID_W), F32).at[:, :, shift:shift + ndc].set(rpb)
    tiles = _toeplitz(w_pad, GRID_W)[..., GRID_W - 1:2 * GRID_W - 1]
    cols = jnp.arange(GRID_W)
    c_start = jnp.clip(cols - NA_WIN_COLS // 2, 0, GRID_W - NA_WIN_COLS)
    valid = (cols[None, :] >= c_start[:, None]) & (cols[None, :] < c_start[:, None] + NA_WIN_COLS)
    tiles = jnp.where(valid[None, None], tiles, MASK_VALUE)
    tiles = jnp.concatenate([tiles, jnp.full((h, 1, GRID_W, GRID_W), MASK_VALUE, F32)], axis=1)
    return jnp.concatenate([tiles, tiles], axis=-1)


def _neighbourhood_attention(proj, tiles, *, batch, seq):
    rows = seq // GRID_W
    assert min(NA_WIN_ROWS, rows) == NA_WIN_ROWS and rows % NA_BLOCK_ROWS == 0 and rows // 2 >= NA_BAND_PAIRS
    kern = functools.partial(_na_kernel, rows=rows)
    pair = 2 * GRID_W
    return pl.pallas_call(
        kern,
        grid=(NA_HEADS, batch),
        in_specs=[
            pl.BlockSpec((seq, HEAD_DIM), lambda h, b: (b, h)),
            pl.BlockSpec((seq, HEAD_DIM), lambda h, b: (b, NA_HEADS + h)),
            pl.BlockSpec((seq, HEAD_DIM), lambda h, b: (b, 2 * NA_HEADS + h)),
            pl.BlockSpec((1, 2 * NA_WIN_ROWS, GRID_W, pair), lambda h, b: (h, 0, 0, 0)),
        ],
        out_specs=pl.BlockSpec((seq, HEAD_DIM), lambda h, b: (b, h)),
        out_shape=jax.ShapeDtypeStruct((batch * seq, NA_HEADS * HEAD_DIM), BF16),
        scratch_shapes=[
            pltpu.VMEM((NA_VARIANTS, GRID_W, NA_WIN_PAIRS * pair), F32),
            pltpu.VMEM((2, NA_BLOCK_ROWS * GRID_W, NA_BAND_PAIRS * pair), F32),
            pltpu.VMEM((2, NA_BLOCK_ROWS * GRID_W, NA_BAND_PAIRS * pair), BF16),
        ],
        compiler_params=pltpu.CompilerParams(
            dimension_semantics=("arbitrary", "arbitrary"), vmem_limit_bytes=VMEM_LIMIT_BYTES),
        name="na_attention",
    )(proj, proj, proj, tiles)


def _t5_bucket(rel):
    nb = REL_BUCKETS // 2
    max_exact = nb // 2
    ret = (rel > 0).astype(jnp.int32) * nb
    n = jnp.abs(rel)
    nf = jnp.maximum(n, 1).astype(jnp.float32)
    large = max_exact + (jnp.log(nf / max_exact) / math.log(REL_MAX_DIST / max_exact) * (nb - max_exact)).astype(jnp.int32)
    large = jnp.minimum(large, nb - 1)
    return ret + jnp.where(n < max_exact, n, large)


def _diff_bias_table(rel_table, tq):
    assert tq >= REL_MAX_DIST
    n = 6 * tq
    rel = jnp.arange(n, dtype=jnp.int32) - (3 * tq - 1)
    bucket = _t5_bucket(rel)
    onehot = bucket[:, None] == jnp.arange(REL_BUCKETS, dtype=jnp.int32)[None, :]
    vec = jnp.sum(jnp.where(onehot[None], rel_table.T[:, None, :], 0.0), axis=-1)
    return _toeplitz(vec.astype(F32), tq)[..., tq - 1:tq - 1 + 5 * tq]


def _diff_kernel(q_ref, k_ref, v_ref, bias_ref, lq1_ref, lk1_ref, lq2_ref, lk2_ref, g_ref, o_ref,
                 s0_ref, s1_ref, m0_ref, m1_ref, p0_ref, p1_ref, l0_ref, l1_ref,
                 *, tq, seq, n_blocks, lambda_init):
    t = pl.program_id(0)
    nq = seq // tq
    nchunk = seq // tq
    i = jnp.minimum(t, n_blocks - 1) % nq

    def fold_lanes(acc, x, op):
        for hh in range(x.shape[1] // LANES):
            part = x[:, hh * LANES:(hh + 1) * LANES]
            acc = part if acc is None else op(acc, part)
        return acc

    @pl.when(t == 0)
    def _():
        s1_ref[...] = jnp.zeros(s1_ref.shape, F32)
        m1_ref[...] = jnp.zeros(m1_ref.shape, F32)
        p1_ref[...] = jnp.zeros(p1_ref.shape, BF16)
        l1_ref[...] = jnp.ones(l1_ref.shape, F32)

    def step(sw_ref, mw_ref, pw_ref, lw_ref, sr_ref, mr_ref, pr_ref, lr_ref):
        lam = (jnp.exp(jnp.sum(lq1_ref[...] * lk1_ref[...], axis=-1, keepdims=True))
               - jnp.exp(jnp.sum(lq2_ref[...] * lk2_ref[...], axis=-1, keepdims=True))
               + lambda_init)
        panel = [pl.multiple_of((jnp.clip(c - i, -2, 2) + 2) * tq, tq) for c in range(nchunk)]
        qs = [q_ref[:, m * DIFF_QK_DIM:(m + 1) * DIFF_QK_DIM] for m in range(2)]
        mprev = [mr_ref[m] for m in range(2)]
        acc = [jnp.dot(pr_ref[m], v_ref[...], preferred_element_type=F32) for m in range(2)]
        out = acc[0] * (1.0 / lr_ref[0]) - acc[1] * (lam / lr_ref[1])
        ms = jnp.mean(out * out, axis=-1, keepdims=True)
        out = out * lax.rsqrt(ms + LN_EPS) * g_ref[...] * (1.0 - lambda_init)
        o_ref[...] = out.astype(o_ref.dtype)
        for m in range(2):
            ls = None
            for c in range(nchunk):
                cols = slice(c * tq, (c + 1) * tq)
                p = jnp.exp2(sr_ref[m, :, cols] - mprev[m])
                ls = fold_lanes(ls, p, jnp.add)
                pw_ref[m, :, cols] = p.astype(BF16)
            lw_ref[m] = jnp.sum(ls, axis=-1, keepdims=True)
        for m in range(2):
            mx = None
            for c in range(nchunk):
                cols = slice(c * tq, (c + 1) * tq)
                kc = k_ref[cols, m * DIFF_QK_DIM:(m + 1) * DIFF_QK_DIM]
                sc = lax.dot_general(qs[m], kc, (((1,), (1,)), ((), ())), preferred_element_type=F32)
                sc = sc + bias_ref[0, :, pl.ds(panel[c], tq)]
                sw_ref[m, :, cols] = sc
                mx = fold_lanes(mx, sc, jnp.maximum)
            mw_ref[m] = jnp.max(mx, axis=-1, keepdims=True)

    @pl.when(t % 2 == 0)
    def _():
        step(s0_ref, m0_ref, p0_ref, l0_ref, s1_ref, m1_ref, p1_ref, l1_ref)

    @pl.when(t % 2 == 1)
    def _():
        step(s1_ref, m1_ref, p1_ref, l1_ref, s0_ref, m0_ref, p0_ref, l0_ref)


def _differential_attention(proj, bias, lq1, lk1, lq2, lk2, subln_g, *, batch, seq, tq, col0, lambda_init):
    nq = seq // tq
    n_blocks = batch * DIFF_HEADS * nq
    qk_w = 2 * DIFF_QK_DIM
    qb0 = col0 // qk_w
    kb0 = qb0 + DIFF_HEADS
    vb0 = kb0 + DIFF_HEADS
    kern = functools.partial(_diff_kernel, tq=tq, seq=seq, n_blocks=n_blocks, lambda_init=lambda_init)

    def split(tt):
        return tt // (DIFF_HEADS * nq), (tt // nq) % DIFF_HEADS, tt % nq

    stage_a = lambda t: split(jnp.minimum(t, n_blocks - 1))
    stage_c = lambda t: split(jnp.clip(t - 2, 0, n_blocks - 1))

    def q_map(t):
        b, h, i = stage_a(t)
        return b * nq + i, qb0 + h

    def k_map(t):
        b, h, _ = stage_a(t)
        return b, kb0 + h

    def bias_map(t):
        _, h, _ = stage_a(t)
        return h, 0, 0

    def v_map(t):
        b, h, _ = stage_c(t)
        return b, vb0 + h

    def o_map(t):
        b, h, i = stage_c(t)
        return b * nq + i, h

    vec = lambda n: pl.BlockSpec((1, n), lambda t: (0, 0))
    return pl.pallas_call(
        kern,
        grid=(n_blocks + 2,),
        in_specs=[
            pl.BlockSpec((tq, qk_w), q_map),
            pl.BlockSpec((seq, qk_w), k_map),
            pl.BlockSpec((seq, DIFF_V_DIM), v_map),
            pl.BlockSpec((1, tq, 5 * tq), bias_map),
            vec(DIFF_QK_DIM), vec(DIFF_QK_DIM), vec(DIFF_QK_DIM), vec(DIFF_QK_DIM),
            vec(DIFF_V_DIM),
        ],
        out_specs=pl.BlockSpec((tq, DIFF_V_DIM), o_map),
        out_shape=jax.ShapeDtypeStruct((batch * seq, DIFF_HEADS * DIFF_V_DIM), BF16),
        scratch_shapes=[pltpu.VMEM((2, tq, seq), F32), pltpu.VMEM((2, tq, seq), F32),
                        pltpu.VMEM((2, tq, 1), F32), pltpu.VMEM((2, tq, 1), F32),
                        pltpu.VMEM((2, tq, seq), BF16), pltpu.VMEM((2, tq, seq), BF16),
                        pltpu.VMEM((2, tq, 1), F32), pltpu.VMEM((2, tq, 1), F32)],
        compiler_params=pltpu.CompilerParams(
            dimension_semantics=("arbitrary",), vmem_limit_bytes=VMEM_LIMIT_BYTES),
        name="diff_attention",
    )(proj, proj, proj, bias, lq1, lk1, lq2, lk2, subln_g)


def _outproj_kernel(na_ref, df_ref, wa_ref, wd_ref, x_ref, gin_ref, bin_ref, g1_ref, b1_ref,
                    of_ref, ob_ref, acc_ref, *, tn):
    j = pl.program_id(1)
    nj = pl.num_programs(1)
    acc = jnp.dot(na_ref[...], wa_ref[...], preferred_element_type=F32)
    acc = acc + jnp.dot(df_ref[...], wd_ref[...], preferred_element_type=F32)
    acc_ref[:, pl.ds(pl.multiple_of(j * tn, tn), tn)] = acc

    @pl.when(j == nj - 1)
    def _():
        h0 = _layer_norm_rows(x_ref[...], gin_ref[...], bin_ref[...])
        y = DEEPNORM_ALPHA * h0 + acc_ref[...]
        h1 = _layer_norm_rows(y, g1_ref[...], b1_ref[...])
        of_ref[...] = h1
        ob_ref[...] = h1.astype(BF16)


def _outproj_ln(na_out, diff_out, w_out, x2, gin, bin_, g1, b1, *, tm, tn):
    m, d = x2.shape
    ka = na_out.shape[1]
    kd = diff_out.shape[1]
    assert ka == kd and w_out.shape == (ka + kd, d)
    kern = functools.partial(_outproj_kernel, tn=tn)
    vec = pl.BlockSpec((1, d), lambda i, j: (0, 0))
    return pl.pallas_call(
        kern,
        grid=(m // tm, d // tn),
        in_specs=[
            pl.BlockSpec((tm, ka), lambda i, j: (i, 0)),
            pl.BlockSpec((tm, kd), lambda i, j: (i, 0)),
            pl.BlockSpec((ka, tn), lambda i, j: (0, j)),
            pl.BlockSpec((kd, tn), lambda i, j: (1, j)),
            pl.BlockSpec((tm, d), lambda i, j: (i, 0)),
            vec, vec, vec, vec,
        ],
        out_specs=[pl.BlockSpec((tm, d), lambda i, j: (i, 0)),
                   pl.BlockSpec((tm, d), lambda i, j: (i, 0))],
        out_shape=[jax.ShapeDtypeStruct((m, d), F32), jax.ShapeDtypeStruct((m, d), BF16)],
        scratch_shapes=[pltpu.VMEM((tm, d), F32)],
        compiler_params=pltpu.CompilerParams(
            dimension_semantics=("arbitrary", "arbitrary"), vmem_limit_bytes=VMEM_LIMIT_BYTES),
        name="outproj_ln",
    )(na_out, diff_out, w_out, w_out, x2, gin, bin_, g1, b1)


def _ffn_kernel(hb_ref, hf_ref, wg_ref, wu_ref, wd_ref, g2_ref, b2_ref, o_ref, acc_ref):
    j = pl.program_id(1)
    nj = pl.num_programs(1)
    @pl.when(j == 0)
    def _():
        acc_ref[...] = jnp.zeros(acc_ref.shape, F32)

    h = hb_ref[...]
    g = jnp.dot(h, wg_ref[...], preferred_element_type=F32)
    u = jnp.dot(h, wu_ref[...], preferred_element_type=F32)
    a = (g * jax.nn.sigmoid(g) * u).astype(BF16)
    acc_ref[...] += jnp.dot(a, wd_ref[...], preferred_element_type=F32)

    @pl.when(j == nj - 1)
    def _():
        y = DEEPNORM_ALPHA * hf_ref[...] + acc_ref[...]
        o_ref[...] = _layer_norm_rows(y, g2_ref[...], b2_ref[...])


def _ffn_ln(h1b, h1f, wg, wu, wd, g2, b2, *, tm, tf):
    m, d = h1f.shape
    f = wg.shape[1]
    vec = pl.BlockSpec((1, d), lambda i, j: (0, 0))
    return pl.pallas_call(
        _ffn_kernel,
        grid=(m // tm, f // tf),
        in_specs=[
            pl.BlockSpec((tm, d), lambda i, j: (i, 0)),
            pl.BlockSpec((tm, d), lambda i, j: (i, 0)),
            pl.BlockSpec((d, tf), lambda i, j: (0, j)),
            pl.BlockSpec((d, tf), lambda i, j: (0, j)),
            pl.BlockSpec((tf, d), lambda i, j: (j, 0)),
            vec, vec,
        ],
        out_specs=pl.BlockSpec((tm, d), lambda i, j: (i, 0)),
        out_shape=jax.ShapeDtypeStruct((m, d), F32),
        scratch_shapes=[pltpu.VMEM((tm, d), F32)],
        compiler_params=pltpu.CompilerParams(
            dimension_semantics=("arbitrary", "arbitrary"), vmem_limit_bytes=VMEM_LIMIT_BYTES),
        name="ffn_ln",
    )(h1b, h1f, wg, wu, wd, g2, b2)


def kernel(x, ln_in_g, ln_in_b, w_in, na_rpb, lambda_q1, lambda_k1, lambda_q2, lambda_k2, diff_subln_g, rel_bias_table, w_out, ln1_g, ln1_b, w_gate, w_up, w_down, ln2_g, ln2_b):
    batch, seq, d = x.shape
    assert w_in.shape[0] == DEPTH == 1
    layer = 0
    na_w = NA_HEADS * HEAD_DIM
    diff_qk_w = DIFF_HEADS * 2 * DIFF_QK_DIM

    x2 = x.reshape(batch * seq, d)
    row = lambda v: v.reshape(1, -1).astype(F32)

    q_cols = ((0, na_w), (3 * na_w, 3 * na_w + diff_qk_w))
    proj = _ln_inproj(x2, row(ln_in_g), row(ln_in_b), w_in[layer].astype(BF16),
                      tm=1024, tn=512, q_col_ranges=q_cols, q_scale=HEAD_DIM ** -0.5 * LOG2E)

    na_out = _neighbourhood_attention(proj, _na_bias_tiles(na_rpb[layer].astype(F32) * LOG2E), batch=batch, seq=seq)

    tq = 256
    lambda_init = 0.8 - 0.6 * math.exp(-0.3 * layer)
    diff_out = _differential_attention(
        proj, _diff_bias_table(rel_bias_table.astype(F32) * LOG2E, tq),
        row(lambda_q1[layer]), row(lambda_k1[layer]), row(lambda_q2[layer]), row(lambda_k2[layer]),
        row(diff_subln_g[layer]), batch=batch, seq=seq, tq=tq, col0=3 * na_w, lambda_init=lambda_init)

    h1f, h1b = _outproj_ln(na_out, diff_out, w_out[layer].astype(BF16), x2, row(ln_in_g), row(ln_in_b),
                           row(ln1_g[layer]), row(ln1_b[layer]), tm=512, tn=512)

    out = _ffn_ln(h1b, h1f, w_gate[layer].astype(BF16), w_up[layer].astype(BF16), w_down[layer].astype(BF16),
                  row(ln2_g[layer]), row(ln2_b[layer]), tm=512, tf=512)
    return out.reshape(batch, seq, d)
```
